```python
import math
import jax
import jax.numpy as jnp
from jax import lax
import numpy as np

D_MODEL = 2048
BATCH = 2
SEQ = 4096
DEPTH = 2
DEC_BATCH = 128
DEC_SEQ = 4
PAST_LEN = 2048
PAGE_SIZE = 128

HGRN_WIDTH = D_MODEL // 4
HGRN_HEAD_DIM = 128
HGRN_HEADS = HGRN_WIDTH // HGRN_HEAD_DIM
HGRN_CHUNK = 64
SSM_WIDTH = D_MODEL // 4
SSM_GROUP_CH = 16
SSM_GROUPS = SSM_WIDTH // SSM_GROUP_CH
SSM_STATE = 64
NSA_WIDTH = D_MODEL - HGRN_WIDTH - SSM_WIDTH
NSA_HEAD_DIM = 64
NSA_HEADS = NSA_WIDTH // NSA_HEAD_DIM
NSA_KV_HEADS = 4
NSA_GROUP = NSA_HEADS // NSA_KV_HEADS
NSA_KV_WIDTH = NSA_KV_HEADS * NSA_HEAD_DIM
CMP_LEN = 32
CMP_STRIDE = 16
CMP_HIDDEN = 256
SEL_BLOCK = 64
SEL_TOPK = 16
SEL_QBLOCK = 64
WINDOW = 512
WIN_QBLOCK = 128
REL_BUCKETS = 32
REL_MAX_DIST = 128
N_EXPERTS = 64
TOP_K = 8
D_EXPERT = 512
D_SHARED = 512
ROUTED_SCALE = 2.5
MOE_BLOCK = 128
RMS_EPS = 1e-6
IN_SPLITS = (HGRN_WIDTH,) * 4 + (SSM_WIDTH, NSA_WIDTH) + (NSA_KV_WIDTH,) * 6 + (3 * NSA_HEADS,)
IN_COLS = sum(IN_SPLITS)

kernel_name = 'hymba_hgrn2_s5_nsa_moe_step'


def rmsnorm(x, gain):
    x32 = x.astype(jnp.float32)
    y = x32 * lax.rsqrt(jnp.mean(x32 * x32, axis=-1, keepdims=True) + RMS_EPS)
    return (y * gain.astype(jnp.float32)).astype(x.dtype)


def rel_bucket(dist):
    exact = REL_BUCKETS // 2
    d = jnp.maximum(dist, 0)
    logd = jnp.log(jnp.maximum(d, 1).astype(jnp.float32) / exact) / math.log(REL_MAX_DIST / exact)
    far = jnp.minimum(exact + (logd * (REL_BUCKETS - exact)).astype(jnp.int32), REL_BUCKETS - 1)
    return jnp.where(d < exact, d, far)


def swiglu(x, wg, wu, wd):
    return (jax.nn.silu(x @ wg) * (x @ wu)) @ wd


def hgrn2_mixer(q, f_logit, i_val, g, lb, s0, norm_w):
    B, T, _ = q.shape
    f = lb + (1.0 - lb) * jax.nn.sigmoid(f_logit.astype(jnp.float32))
    log_f = jnp.log(f)
    k = 1.0 - f
    C = min(HGRN_CHUNK, T)
    n = -(-T // C)
    pad = n * C - T

    def blocks(a):
        a = jnp.pad(a.astype(jnp.float32), ((0, 0), (0, pad), (0, 0)))
        return a.reshape(B, n, C, HGRN_HEADS, HGRN_HEAD_DIM).transpose(1, 0, 3, 2, 4)

    causal = jnp.tril(jnp.ones((C, C), dtype=bool))[:, :, None]

    def step(S, inp):
        qc, kc, vc, lc = inp
        b = jnp.cumsum(lc, axis=2)
        decay = jnp.exp(jnp.where(causal, b[:, :, :, None, :] - b[:, :, None, :, :], -jnp.inf))
        a = jnp.einsum('bhtd,bhsd,bhtsd->bhts', qc, kc, decay)
        o = jnp.einsum('bhts,bhsv->bhtv', a, vc) + jnp.einsum('bhtd,bhdv->bhtv', qc * jnp.exp(b), S)
        b_end = b[:, :, -1:, :]
        S = jnp.exp(b_end[:, :, 0, :])[..., None] * S + jnp.einsum('bhsd,bhsv->bhdv', kc * jnp.exp(b_end - b), vc)
        return S, o

    S, o = lax.scan(step, s0.astype(jnp.float32), (blocks(q), blocks(k), blocks(i_val), blocks(log_f)))
    o = o.transpose(1, 0, 3, 2, 4).reshape(B, n * C, HGRN_HEADS, HGRN_HEAD_DIM)[:, :T]
    o = o * lax.rsqrt(jnp.mean(o * o, axis=-1, keepdims=True) + RMS_EPS) * norm_w.astype(jnp.float32)
    o = o.reshape(B, T, HGRN_WIDTH) * jax.nn.silu(g.astype(jnp.float32))
    return o.astype(q.dtype), S


def s5_mixer(u, h0, lam_re, lam_im, log_dt, b_re, b_im, c_re, c_im, d_skip, w_glu, b_glu):
    B, T, _ = u.shape
    f32 = jnp.float32
    lr, li = lam_re.astype(f32), lam_im.astype(f32)
    dt = jnp.exp(log_dt.astype(f32))[:, None]
    mag = jnp.exp(lr * dt)
    ar, ai = mag * jnp.cos(li * dt), mag * jnp.sin(li * dt)
    den = lr * lr + li * li
    fr = ((ar - 1.0) * lr + ai * li) / den
    fi = (ai * lr - (ar - 1.0) * li) / den
    br, bi = b_re.astype(f32), b_im.astype(f32)
    bbr = fr[..., None] * br - fi[..., None] * bi
    bbi = fr[..., None] * bi + fi[..., None] * br
    ug = u.astype(f32).reshape(B, T, SSM_GROUPS, SSM_GROUP_CH)
    xr = jnp.einsum('btgc,gpc->btgp', ug, bbr)
    xi = jnp.einsum('btgc,gpc->btgp', ug, bbi)
    h0r, h0i = h0[..., 0].astype(f32), h0[..., 1].astype(f32)
    xr = xr.at[:, 0].add(ar * h0r - ai * h0i)
    xi = xi.at[:, 0].add(ar * h0i + ai * h0r)
    shape = xr.shape

    def combine(e1, e2):
        a1r, a1i, b1r, b1i = e1
        a2r, a2i, b2r, b2i = e2
        return (a1r * a2r - a1i * a2i, a1r * a2i + a1i * a2r,
                a2r * b1r - a2i * b1i + b2r, a2r * b1i + a2i * b1r + b2i)

    _, _, hr, hi = lax.associative_scan(
        combine, (jnp.broadcast_to(ar, shape), jnp.broadcast_to(ai, shape), xr, xi), axis=1)
    y = jnp.einsum('btgp,gcp->btgc', hr, c_re.astype(f32)) - jnp.einsum('btgp,gcp->btgc', hi, c_im.astype(f32))
    y = y.reshape(B, T, SSM_WIDTH) + d_skip.astype(f32) * u.astype(f32)
    y = jax.nn.gelu(y).astype(u.dtype)
    y = y * jax.nn.sigmoid(y @ w_glu + b_glu)
    return y, jnp.stack([hr[:, -1], hi[:, -1]], axis=-1)


def compress_rows(rows, pe, w1, w2):
    B, L = rows.shape[:2]
    n_sub = CMP_LEN // CMP_STRIDE
    nseg = L // CMP_STRIDE
    nc = nseg - n_sub + 1
    seg = rows[:, :nseg * CMP_STRIDE].astype(jnp.float32).reshape(B, nseg, CMP_STRIDE, NSA_KV_HEADS, NSA_HEAD_DIM)
    seg = seg.transpose(0, 1, 3, 2, 4).reshape(B, nseg, NSA_KV_HEADS, CMP_STRIDE * NSA_HEAD_DIM)
    w1 = w1.astype(jnp.float32)
    w1s = w1.reshape(n_sub, CMP_STRIDE * NSA_HEAD_DIM, CMP_HIDDEN)
    hid = pe.astype(jnp.float32).reshape(-1) @ w1
    for j in range(n_sub):
        hid = hid + jnp.einsum('bnhx,xf->bnhf', seg[:, j:j + nc], w1s[j])
    return jax.nn.gelu(hid) @ w2.astype(jnp.float32)


def cmp_to_sel(nc, nsel):
    cs = jnp.arange(nc) * CMP_STRIDE
    ss = jnp.arange(nsel) * SEL_BLOCK
    ov = jnp.minimum(cs[:, None] + CMP_LEN, ss[None, :] + SEL_BLOCK) - jnp.maximum(cs[:, None], ss[None, :])
    return jnp.clip(ov, 0, None).astype(jnp.float32) / CMP_LEN


def selected_attention(qf, sel_idx, k_rows, v_rows, t_pos, rel_bias, nsel):
    B, T = qf.shape[:2]
    L = k_rows.shape[1]
    n_sel = sel_idx.shape[-1]

    def to_blocks(r):
        r = jnp.pad(r.astype(jnp.float32), ((0, 0), (0, nsel * SEL_BLOCK - L), (0, 0), (0, 0)))
        return r.reshape(B, nsel, SEL_BLOCK, NSA_KV_HEADS, NSA_HEAD_DIM).transpose(0, 3, 1, 2, 4)

    kb, vb = to_blocks(k_rows), to_blocks(v_rows)
    qb_len = min(SEL_QBLOCK, T)
    nqb = -(-T // qb_len)
    tp = nqb * qb_len - T
    q_blk = jnp.pad(qf, ((0, 0), (0, tp), (0, 0), (0, 0), (0, 0)))
    q_blk = q_blk.reshape(B, nqb, qb_len, NSA_KV_HEADS, NSA_GROUP, NSA_HEAD_DIM).transpose(1, 0, 3, 4, 2, 5)
    i_blk = jnp.pad(sel_idx, ((0, 0), (0, 0), (0, tp), (0, 0)))
    i_blk = i_blk.reshape(B, NSA_KV_HEADS, nqb, qb_len, n_sel).transpose(2, 0, 1, 3, 4)
    p_blk = jnp.pad(t_pos, (0, tp), mode='edge').reshape(nqb, qb_len)
    tbl = rel_bias.reshape(REL_BUCKETS, NSA_KV_HEADS, NSA_GROUP).transpose(1, 2, 0)
    h_ix = jnp.arange(NSA_KV_HEADS)[None, :, None, None, None]
    g_ix = jnp.arange(NSA_GROUP)[None, None, :, None, None]
    offs = jnp.arange(SEL_BLOCK)
    gather = jax.vmap(jax.vmap(lambda blocks, ids: blocks[ids]))
    n_keys = n_sel * SEL_BLOCK

    def one_block(args):
        qb, ib, pb = args
        kg = gather(kb, ib).reshape(B, NSA_KV_HEADS, qb_len, n_keys, NSA_HEAD_DIM)
        vg = gather(vb, ib).reshape(B, NSA_KV_HEADS, qb_len, n_keys, NSA_HEAD_DIM)
        kpos = (ib[..., None] * SEL_BLOCK + offs).reshape(B, NSA_KV_HEADS, qb_len, n_keys)
        dist = pb[None, None, :, None] - kpos
        bias = tbl[h_ix, g_ix, rel_bucket(dist)[:, :, None]]
        logit = jnp.einsum('bkgqd,bkqsd->bkgqs', qb, kg) + bias.astype(jnp.float32)
        logit = jnp.where((dist >= 0)[:, :, None], logit, -1e30)
        return jnp.einsum('bkgqs,bkqsd->bkgqd', jax.nn.softmax(logit, axis=-1), vg)

    o = lax.map(one_block, (q_blk, i_blk, p_blk))
    return o.transpose(1, 0, 4, 2, 3, 5).reshape(B, nqb * qb_len, NSA_KV_HEADS, NSA_GROUP, NSA_HEAD_DIM)[:, :T]


def window_attention(qf, k_all, v_all, pos0, rel_bias):
    B, T = qf.shape[:2]
    qb_len = min(WIN_QBLOCK, T)
    nqb = -(-T // qb_len)
    tp = nqb * qb_len - T
    span = WINDOW + qb_len
    pad4 = ((0, 0), (0, tp), (0, 0), (0, 0))
    k_all = jnp.pad(k_all.astype(jnp.float32), pad4)
    v_all = jnp.pad(v_all.astype(jnp.float32), pad4)
    q_blk = jnp.pad(qf, ((0, 0), (0, tp), (0, 0), (0, 0), (0, 0)))
    q_blk = q_blk.reshape(B, nqb, qb_len, NSA_KV_HEADS, NSA_GROUP, NSA_HEAD_DIM)
    rows = jnp.arange(nqb)[:, None] * qb_len + jnp.arange(span)[None, :]
    kb, vb = k_all[:, rows], v_all[:, rows]
    q_rel = jnp.arange(nqb)[:, None] * qb_len + jnp.arange(qb_len)[None, :]
    dist = q_rel[:, :, None] - (rows - WINDOW)[:, None, :]
    ok = (dist >= 0) & (dist <= WINDOW) & ((pos0 - WINDOW + rows) >= 0)[:, None, :]
    bias = rel_bias[rel_bucket(dist)].reshape(nqb, qb_len, span, NSA_KV_HEADS, NSA_GROUP).transpose(0, 3, 4, 1, 2)
    logit = jnp.einsum('bnqkgd,bnskd->bnkgqs', q_blk, kb) + bias[None].astype(jnp.float32)
    logit = jnp.where(ok[None, :, None, None], logit, -1e30)
    o = jnp.einsum('bnkgqs,bnskd->bnqkgd', jax.nn.softmax(logit, axis=-1), vb)
    return o.reshape(B, nqb * qb_len, NSA_KV_HEADS, NSA_GROUP, NSA_HEAD_DIM)[:, :T]


def nsa_mixer(q, gate_logits, kv_full, win_all, pos0, cmp_pe, cmp_w1, cmp_w2, rel_bias):
    B, T = q.shape[:2]
    L = kv_full.shape[1]
    qf = q.astype(jnp.float32).reshape(B, T, NSA_KV_HEADS, NSA_GROUP, NSA_HEAD_DIM) * NSA_HEAD_DIM ** -0.5
    t_pos = pos0 + jnp.arange(T)
    kc = compress_rows(kv_full[:, :, 0], cmp_pe[0], cmp_w1[0], cmp_w2[0])
    vc = compress_rows(kv_full[:, :, 1], cmp_pe[1], cmp_w1[1], cmp_w2[1])
    nc = kc.shape[1]
    c_ok = (jnp.arange(nc) * CMP_STRIDE + CMP_LEN - 1)[None, :] <= t_pos[:, None]
    logit = jnp.einsum('btkgd,bnkd->bkgtn', qf, kc)
    p_cmp = jnp.where(c_ok, jax.nn.softmax(jnp.where(c_ok, logit, -1e30), axis=-1), 0.0)
    o_cmp = jnp.einsum('bkgtn,bnkd->btkgd', p_cmp, vc)
    nsel = -(-L // SEL_BLOCK)
    imp = jnp.einsum('bkgtn,nj->bktj', p_cmp, cmp_to_sel(nc, nsel))
    blk = jnp.arange(nsel)[None, :]
    cur = (t_pos // SEL_BLOCK)[:, None]
    forced = (blk == 0) | (blk == cur) | (blk == cur - 1)
    allowed = blk * SEL_BLOCK <= t_pos[:, None]
    score = jnp.where(forced, NSA_GROUP + 1.0, jnp.where(allowed, imp, -1.0))
    _, sel_idx = lax.top_k(score, min(SEL_TOPK, nsel))
    o_sel = selected_attention(qf, sel_idx, kv_full[:, :, 2], kv_full[:, :, 3], t_pos, rel_bias, nsel)
    o_win = window_attention(qf, win_all[:, :, 0], win_all[:, :, 1], pos0, rel_bias)
    g = jax.nn.sigmoid(gate_logits.astype(jnp.float32)).reshape(B, T, 3, NSA_KV_HEADS, NSA_GROUP, 1)
    o = g[:, :, 0] * o_cmp + g[:, :, 1] * o_sel + g[:, :, 2] * o_win
    return o.reshape(B, T, NSA_WIDTH).astype(q.dtype)


def routed_experts(xt, idx, wts, w_gate, w_up, w_down):
    N, D = xt.shape
    A = N * TOP_K
    flat_e = idx.reshape(-1)
    flat_tok = jnp.repeat(jnp.arange(N), TOP_K)
    flat_w = wts.reshape(-1)
    order = jnp.argsort(flat_e)
    sorted_e = flat_e[order]
    counts = jnp.bincount(flat_e, length=N_EXPERTS)
    padded = (counts + MOE_BLOCK - 1) // MOE_BLOCK * MOE_BLOCK
    pad_end = jnp.cumsum(padded)
    pad_start = pad_end - padded
    start = jnp.cumsum(counts) - counts
    dest = pad_start[sorted_e] + jnp.arange(A) - start[sorted_e]
    n_blocks = -(-A // MOE_BLOCK) + N_EXPERTS
    n_slots = n_blocks * MOE_BLOCK
    slot_tok = jnp.full((n_slots,), N, jnp.int32).at[dest].set(flat_tok[order].astype(jnp.int32))
    slot_w = jnp.zeros((n_slots,), jnp.float32).at[dest].set(flat_w[order])
    blk_exp = jnp.minimum(jnp.searchsorted(pad_end, jnp.arange(n_blocks) * MOE_BLOCK, side='right'), N_EXPERTS - 1)
    x_pad = jnp.concatenate([xt, jnp.zeros((1, D), xt.dtype)], axis=0)

    def expert_block(args):
        tok, e = args
        return swiglu(x_pad[tok], w_gate[e], w_up[e], w_down[e])

    yb = lax.map(expert_block, (slot_tok.reshape(n_blocks, MOE_BLOCK), blk_exp)).reshape(n_slots, D)
    yb = yb * slot_w[:, None].astype(yb.dtype)
    return jax.ops.segment_sum(yb, slot_tok, num_segments=N + 1)[:N]


def moe_ffn(h, router_w, router_bias, w_gate, w_up, w_down, sw_gate, sw_up, sw_down):
    B, T, D = h.shape
    xt = h.reshape(B * T, D)
    scores = jax.nn.sigmoid(xt.astype(jnp.float32) @ router_w.astype(jnp.float32))
    _, idx = lax.top_k(scores + router_bias.astype(jnp.float32), TOP_K)
    wts = jnp.take_along_axis(scores, idx, axis=-1)
    wts = wts / jnp.sum(wts, axis=-1, keepdims=True) * ROUTED_SCALE
    y = routed_experts(xt, idx, wts, w_gate, w_up, w_down) + swiglu(xt, sw_gate, sw_up, sw_down)
    return y.reshape(B, T, D)


def decoder_layer(x, c, lp, lb, rel_bias, hgrn0, ssm0, kv_past, win_past, pos0):
    B, T, _ = x.shape
    mod = jax.nn.silu(c) @ lp['w_ada'] + lp['b_ada']
    sh1, sc1, g1, sh2, sc2, g2 = jnp.split(mod[:, None, :], 6, axis=-1)
    h = rmsnorm(x, lp['norm_mix']) * (1.0 + sc1) + sh1
    cuts = np.cumsum(IN_SPLITS)[:-1].tolist()
    hq, hf, hi, hg, su, nq, k_c, v_c, k_s, v_s, k_w, v_w, n_g = jnp.split(h @ lp['w_in'], cuts, axis=-1)
    o_h, hgrn_new = hgrn2_mixer(hq, hf, hi, hg, lb, hgrn0, lp['hgrn_norm'])
    o_s, ssm_new = s5_mixer(su, ssm0, lp['lam_re'], lp['lam_im'], lp['log_dt'], lp['b_re'], lp['b_im'],
                            lp['c_re'], lp['c_im'], lp['d_skip'], lp['w_glu'], lp['b_glu'])

    def heads(a):
        return a.reshape(B, T, NSA_KV_HEADS, NSA_HEAD_DIM)

    new_rows = jnp.stack([heads(k_c), heads(v_c), heads(k_s), heads(v_s)], axis=2)
    new_win = jnp.stack([heads(k_w), heads(v_w)], axis=2)
    if kv_past is None:
        kv_full = new_rows
    else:
        kv_full = jnp.concatenate([kv_past.astype(new_rows.dtype), new_rows], axis=1)
    if win_past is None:
        win_past = jnp.zeros((B, 0) + new_win.shape[2:], new_win.dtype)
    win_prefix = jnp.pad(win_past.astype(new_win.dtype),
                         ((0, 0), (WINDOW - win_past.shape[1], 0), (0, 0), (0, 0), (0, 0)))
    win_all = jnp.concatenate([win_prefix, new_win], axis=1)
    o_n = nsa_mixer(nq.reshape(B, T, NSA_HEADS, NSA_HEAD_DIM), n_g, kv_full, win_all, pos0,
                    lp['cmp_pe'], lp['cmp_w1'], lp['cmp_w2'], rel_bias)
    mix = jnp.concatenate([o_h, o_s, o_n], axis=-1) @ lp['w_out']
    x = x + g1 * mix
    h = rmsnorm(x, lp['norm_ffn']) * (1.0 + sc2) + sh2
    x = x + g2 * moe_ffn(h, lp['router_w'], lp['router_bias'], lp['exp_w_gate'], lp['exp_w_up'],
                         lp['exp_w_down'], lp['sh_w_gate'], lp['sh_w_up'], lp['sh_w_down'])
    win_len = min(WINDOW, PAST_LEN)
    return x, new_rows, win_all[:, win_all.shape[1] - win_len:], hgrn_new, ssm_new


def setup_inputs(seed: int = 0) -> dict:
    key = jax.random.key(seed)
    keys = iter(jax.random.split(key, 48))

    def nrm(shape, scale):
        return jax.random.normal(next(keys), shape, jnp.float32) * scale

    def gain(shape):
        return 1.0 + nrm(shape, 0.02)

    n_pages = PAST_LEN // PAGE_SIZE
    used = DEC_BATCH * n_pages
    n_pool = used + max(1, used // 4)
    win_len = min(WINDOW, PAST_LEN)
    kvh, hd = NSA_KV_HEADS, NSA_HEAD_DIM
    page_table = jax.random.permutation(next(keys), n_pool)[:used].reshape(DEC_BATCH, n_pages).astype(jnp.int32)
    lam_im0 = jnp.pi * jnp.arange(SSM_STATE, dtype=jnp.float32)
    return {
        'x_prompt': nrm((BATCH, SEQ, D_MODEL), 1.0),
        'x_sample': nrm((DEC_BATCH, DEC_SEQ, D_MODEL), 1.0),
        'c_prompt': nrm((BATCH, D_MODEL), 1.0),
        'c_sample': nrm((DEC_BATCH, D_MODEL), 1.0),
        'cache_kv': nrm((DEPTH, n_pool, PAGE_SIZE, 4, kvh, hd), 1.0),
        'page_table': page_table,
        'cache_win': nrm((DEPTH, DEC_BATCH, win_len, 2, kvh, hd), 1.0),
        'state_hgrn': nrm((DEPTH, DEC_BATCH, HGRN_HEADS, HGRN_HEAD_DIM, HGRN_HEAD_DIM), 0.3),
        'state_ssm': nrm((DEPTH, DEC_BATCH, SSM_GROUPS, SSM_STATE, 2), 0.3),
        'w_ada': nrm((DEPTH, D_MODEL, 6 * D_MODEL), 0.5 * D_MODEL ** -0.5),
        'b_ada': nrm((DEPTH, 6 * D_MODEL), 0.02),
        'norm_mix': gain((DEPTH, D_MODEL)),
        'norm_ffn': gain((DEPTH, D_MODEL)),
        'w_in': nrm((DEPTH, D_MODEL, IN_COLS), D_MODEL ** -0.5),
        'w_out': nrm((DEPTH, D_MODEL, D_MODEL), D_MODEL ** -0.5),
        'hgrn_lb_logits': nrm((DEPTH, HGRN_WIDTH), 0.5),
        'hgrn_norm': gain((DEPTH, HGRN_HEAD_DIM)),
        'ssm_lambda_re': -0.5 + nrm((DEPTH, SSM_GROUPS, SSM_STATE), 0.01),
        'ssm_lambda_im': lam_im0 + nrm((DEPTH, SSM_GROUPS, SSM_STATE), 0.01),
        'ssm_log_dt': jax.random.uniform(next(keys), (DEPTH, SSM_GROUPS), jnp.float32, math.log(1e-3), math.log(1e-1)),
        'ssm_b_re': nrm((DEPTH, SSM_GROUPS, SSM_STATE, SSM_GROUP_CH), (2 * SSM_GROUP_CH) ** -0.5),
        'ssm_b_im': nrm((DEPTH, SSM_GROUPS, SSM_STATE, SSM_GROUP_CH), (2 * SSM_GROUP_CH) ** -0.5),
        'ssm_c_re': nrm((DEPTH, SSM_GROUPS, SSM_GROUP_CH, SSM_STATE), (2 * SSM_STATE) ** -0.5),
        'ssm_c_im': nrm((DEPTH, SSM_GROUPS, SSM_GROUP_CH, SSM_STATE), (2 * SSM_STATE) ** -0.5),
        'ssm_d': nrm((DEPTH, SSM_WIDTH), 1.0),
        'ssm_w_glu': nrm((DEPTH, SSM_WIDTH, SSM_WIDTH), SSM_WIDTH ** -0.5),
        'ssm_b_glu': nrm((DEPTH, SSM_WIDTH), 0.02),
        'cmp_pe': nrm((DEPTH, 2, CMP_LEN, NSA_HEAD_DIM), 0.5),
        'cmp_w1': nrm((DEPTH, 2, CMP_LEN * NSA_HEAD_DIM, CMP_HIDDEN), (CMP_LEN * NSA_HEAD_DIM) ** -0.5),
        'cmp_w2': nrm((DEPTH, 2, CMP_HIDDEN, NSA_HEAD_DIM), CMP_HIDDEN ** -0.5),
        'rel_bias': nrm((REL_BUCKETS, NSA_HEADS), 0.5),
        'router_w': nrm((DEPTH, D_MODEL, N_EXPERTS), D_MODEL ** -0.5),
        'router_bias': nrm((DEPTH, N_EXPERTS), 0.01),
        'exp_w_gate': nrm((DEPTH, N_EXPERTS, D_MODEL, D_EXPERT), D_MODEL ** -0.5),
        'exp_w_up': nrm((DEPTH, N_EXPERTS, D_MODEL, D_EXPERT), D_MODEL ** -0.5),
        'exp_w_down': nrm((DEPTH, N_EXPERTS, D_EXPERT, D_MODEL), D_EXPERT ** -0.5),
        'sh_w_gate': nrm((DEPTH, D_MODEL, D_SHARED), D_MODEL ** -0.5),
        'sh_w_up': nrm((DEPTH, D_MODEL, D_SHARED), D_MODEL ** -0.5),
        'sh_w_down': nrm((DEPTH, D_SHARED, D_MODEL), D_SHARED ** -0.5),
        'final_norm': gain((D_MODEL,)),
    }


def reference(x_prompt, x_sample, c_prompt, c_sample, cache_kv, page_table, cache_win, state_hgrn, state_ssm,
              w_ada, b_ada, norm_mix, norm_ffn, w_in, w_out, hgrn_lb_logits, hgrn_norm,
              ssm_lambda_re, ssm_lambda_im, ssm_log_dt, ssm_b_re, ssm_b_im, ssm_c_re, ssm_c_im, ssm_d,
              ssm_w_glu, ssm_b_glu, cmp_pe, cmp_w1, cmp_w2, rel_bias, router_w, router_bias,
              exp_w_gate, exp_w_up, exp_w_down, sh_w_gate, sh_w_up, sh_w_down, final_norm):
    p_lb = jax.nn.softmax(hgrn_lb_logits.astype(jnp.float32), axis=0)
    lower = jnp.cumsum(p_lb, axis=0) - p_lb[0]
    n_pages = PAST_LEN // PAGE_SIZE
    dec_b = x_sample.shape[0]
    xp, xs = x_prompt, x_sample
    rows_p, win_p, hg_p, ss_p = [], [], [], []
    rows_s, win_s, hg_s, ss_s = [], [], [], []
    for l in range(DEPTH):
        lp = {
            'w_ada': w_ada[l], 'b_ada': b_ada[l], 'norm_mix': norm_mix[l], 'norm_ffn': norm_ffn[l],
            'w_in': w_in[l], 'w_out': w_out[l], 'hgrn_norm': hgrn_norm[l],
            'lam_re': ssm_lambda_re[l], 'lam_im': ssm_lambda_im[l], 'log_dt': ssm_log_dt[l],
            'b_re': ssm_b_re[l], 'b_im': ssm_b_im[l], 'c_re': ssm_c_re[l], 'c_im': ssm_c_im[l],
            'd_skip': ssm_d[l], 'w_glu': ssm_w_glu[l], 'b_glu': ssm_b_glu[l],
            'cmp_pe': cmp_pe[l], 'cmp_w1': cmp_w1[l], 'cmp_w2': cmp_w2[l],
            'router_w': router_w[l], 'router_bias': router_bias[l],
            'exp_w_gate': exp_w_gate[l], 'exp_w_up': exp_w_up[l], 'exp_w_down': exp_w_down[l],
            'sh_w_gate': sh_w_gate[l], 'sh_w_up': sh_w_up[l], 'sh_w_down': sh_w_down[l],
        }
        bp = xp.shape[0]
        hg0 = jnp.zeros((bp, HGRN_HEADS, HGRN_HEAD_DIM, HGRN_HEAD_DIM), jnp.float32)
        ss0 = jnp.zeros((bp, SSM_GROUPS, SSM_STATE, 2), jnp.float32)
        xp, r, w, hgs, sss = decoder_layer(xp, c_prompt, lp, lower[l], rel_bias, hg0, ss0, None, None, 0)
        rows_p.append(r)
        win_p.append(w)
        hg_p.append(hgs)
        ss_p.append(sss)
        past = cache_kv[l][page_table].reshape((dec_b, n_pages * PAGE_SIZE) + cache_kv.shape[3:])
        xs, r, w, hgs, sss = decoder_layer(xs, c_sample, lp, lower[l], rel_bias, state_hgrn[l], state_ssm[l],
                                           past, cache_win[l], PAST_LEN)
        rows_s.append(r)
        win_s.append(w)
        hg_s.append(hgs)
        ss_s.append(sss)
    y_prompt = rmsnorm(xp, final_norm)
    y_sample = rmsnorm(xs, final_norm)
    return (y_prompt, y_sample, jnp.stack(rows_p), jnp.stack(win_p), jnp.stack(hg_p), jnp.stack(ss_p),
            jnp.stack(rows_s), jnp.stack(win_s), jnp.stack(hg_s), jnp.stack(ss_s))
```

```python
import functools
import math

import numpy as np
import jax
import jax.numpy as jnp
from jax import lax
from jax.experimental import pallas as pl
from jax.experimental.pallas import tpu as pltpu

F32 = jnp.float32
BF16 = jnp.bfloat16

D = 2048
B_P, T_P = 2, 4096
B_S, T_S = 128, 4
DEPTH = 2
PAST = 2048
PAGE = 128
N_PAGES = PAST // PAGE
NP = B_P * T_P
NS = B_S * T_S
NTOK = NP + NS

HG_W, HG_HD, HG_H, HG_C = 512, 128, 4, 64
SSM_W, SSM_CH, SSM_G, SSM_P = 512, 16, 32, 64
NSA_W, HD, NH, KVH, GRP = 1024, 64, 16, 4, 4
KV_W = KVH * HD
CMP_LEN, CMP_STRIDE, CMP_HID = 32, 16, 256
SEL_BLOCK, SEL_TOPK = 64, 16
WINDOW = 512
REL_BUCKETS, REL_MAX_DIST = 32, 128
N_EXP, TOP_K, D_EXP = 64, 8, 512
ROUTED_SCALE = 2.5
EPS = 1e-6
NEG = -1e30

C_Q = 0
C_HG = 1024
C_SU = 3072
C_KV = 3584
C_NG = 5120
Z_COLS = 5632
LANE = 128

VMEM_LIMIT = 56 * 1024 * 1024


def _cp(sem, vmem=VMEM_LIMIT):
    return pltpu.CompilerParams(dimension_semantics=sem, vmem_limit_bytes=vmem)


def _dot(a, b):
    return jnp.dot(a, b, preferred_element_type=F32)


def _dot_nt(a, b):
    return lax.dot_general(a, b, (((1,), (1,)), ((), ())), preferred_element_type=F32)


def _dot_tn(a, b):
    return lax.dot_general(a, b, (((0,), (0,)), ((), ())), preferred_element_type=F32)


def _split3(x):
    x1 = x.astype(BF16)
    r1 = x - x1.astype(F32)
    x2 = r1.astype(BF16)
    x3 = (r1 - x2.astype(F32)).astype(BF16)
    return x1, x2, x3


def _sigmoid(x):
    return 1.0 / (1.0 + jnp.exp(-x))


def _silu(x):
    return x * _sigmoid(x)


def _gelu(x):
    return 0.5 * x * (1.0 + jnp.tanh(math.sqrt(2.0 / math.pi) * (x + 0.044715 * (x * x * x))))


def _iota(shape, dim):
    return lax.broadcasted_iota(jnp.int32, shape, dim)


def _mm_kernel(*refs, pre, has_bias, epi):
    x_ref, w_ref = refs[0], refs[1]
    o_ref = refs[-1]
    x = x_ref[...].astype(F32)
    xin = _silu(x) if pre == "silu" else x
    acc = _dot(xin.astype(BF16), w_ref[...].astype(BF16))
    if has_bias:
        acc = acc + refs[2][...]
    if epi == "glu":
        acc = x * _sigmoid(acc)
    o_ref[...] = acc.astype(o_ref.dtype)


def _mm(x, w, bias=None, *, pre=None, epi=None, tm, tn, name):
    m, k = x.shape
    n = w.shape[1]
    ins = [x, w]
    specs = [pl.BlockSpec((tm, k), lambda i, j: (i, 0)), pl.BlockSpec((k, tn), lambda i, j: (0, j))]
    if bias is not None:
        ins.append(bias)
        specs.append(pl.BlockSpec((1, tn), lambda i, j: (0, j)))
    return pl.pallas_call(
        functools.partial(_mm_kernel, pre=pre, has_bias=bias is not None, epi=epi),
        out_shape=jax.ShapeDtypeStruct((m, n), F32),
        grid=(m // tm, n // tn),
        in_specs=specs,
        out_specs=pl.BlockSpec((tm, tn), lambda i, j: (i, j)),
        compiler_params=_cp(("parallel", "parallel")),
        name=name,
    )(*ins)


def _mod_specs(tm):
    npt, tpb, nst = NP // tm, T_P // tm, NS // tm
    pspec = pl.BlockSpec((1, 1, D), lambda i, *_: (jnp.minimum(i // tpb, B_P - 1), 0, 0))
    sspec = pl.BlockSpec((tm, D), lambda i, *_: (jnp.clip(i - npt, 0, nst - 1), 0))
    return pspec, sspec, npt


def _pick_mod(i, npt, p_ref, s_ref):
    return jnp.where(i < npt, p_ref[0], s_ref[...])


def _rms(x):
    return x * lax.rsqrt(jnp.mean(x * x, axis=-1, keepdims=True) + EPS)


def _inproj_kernel(x_ref, gain_ref, scp_ref, scs_ref, shp_ref, shs_ref, w_ref, o_ref, h_scr, *, npt):
    i = pl.program_id(0)

    @pl.when(pl.program_id(1) == 0)
    def _():
        y = _rms(x_ref[...]) * gain_ref[...]
        sc = _pick_mod(i, npt, scp_ref, scs_ref)
        sh = _pick_mod(i, npt, shp_ref, shs_ref)
        h_scr[...] = (y * (1.0 + sc) + sh).astype(BF16)

    o_ref[...] = _dot(h_scr[...], w_ref[...])


def _inproj(x, gain, scp, scs, shp, shs, w):
    tm, tn = 512, 512
    pspec, sspec, npt = _mod_specs(tm)
    return pl.pallas_call(
        functools.partial(_inproj_kernel, npt=npt),
        out_shape=jax.ShapeDtypeStruct((NTOK, Z_COLS), F32),
        grid=(NTOK // tm, Z_COLS // tn),
        in_specs=[pl.BlockSpec((tm, D), lambda i, j: (i, 0)), pl.BlockSpec((1, D), lambda i, j: (0, 0)),
                  pspec, sspec, pspec, sspec, pl.BlockSpec((D, tn), lambda i, j: (0, j))],
        out_specs=pl.BlockSpec((tm, tn), lambda i, j: (i, j)),
        scratch_shapes=[pltpu.VMEM((tm, D), BF16)],
        compiler_params=_cp(("parallel", "arbitrary")),
        name="inproj",
    )(x, gain, scp, scs, shp, shs, w)


def _hgrn_kernel(*refs, t_valid, has_s0):
    if has_s0:
        q_ref, f_ref, i_ref, g_ref, lb_ref, nw_ref, s0_ref, o_ref, sout_ref, s_scr, oi_scr = refs
    else:
        q_ref, f_ref, i_ref, g_ref, lb_ref, nw_ref, o_ref, sout_ref, s_scr, oi_scr = refs
    C = HG_C
    c = pl.program_id(2)

    @pl.when(c == 0)
    def _():
        if has_s0:
            s_scr[...] = s0_ref[0, 0]
        else:
            s_scr[...] = jnp.zeros((HG_HD, HG_HD), F32)

    lb = lb_ref[...]
    f = lb + (1.0 - lb) * _sigmoid(f_ref[...])
    lf = jnp.log(f)
    k = 1.0 - f
    v = i_ref[...]
    q = q_ref[...]
    row = _iota((C, HG_HD), 0)
    if t_valid < C:
        ok = row < t_valid
        lf = jnp.where(ok, lf, 0.0)
        k = jnp.where(ok, k, 0.0)
        v = jnp.where(ok, v, 0.0)
    tri = (_iota((C, C), 1) <= _iota((C, C), 0)).astype(BF16)
    l1, l2, l3 = _split3(lf)
    b = _dot(tri, l1) + _dot(tri, l2) + _dot(tri, l3)
    s_prev = s_scr[...]
    o_inter = _dot((q * jnp.exp(b)).astype(BF16), s_prev.astype(BF16))
    ones = jnp.ones((HG_HD, HG_HD), BF16)
    n_blk = -(-t_valid // 8)
    for tb in range(n_blk):
        ns = (tb + 1) * 8
        bs, ks, vs, rs = b[:ns], k[:ns], v[:ns], row[:ns]
        es = []
        for ii in range(8):
            t = tb * 8 + ii
            arg = jnp.where(rs <= t, b[t:t + 1] - bs, NEG)
            es.append(jnp.exp(arg) * (ks * q[t:t + 1]))
        e_all = jnp.concatenate(es, axis=0).astype(BF16)
        a_all = _dot(e_all, ones)
        for ii in range(8):
            t = tb * 8 + ii
            oi_scr[t:t + 1, :] = jnp.sum(a_all[ii * ns:(ii + 1) * ns] * vs, axis=0, keepdims=True)
    if n_blk * 8 < C:
        oi_scr[n_blk * 8:, :] = jnp.zeros((C - n_blk * 8, HG_HD), F32)
    o = oi_scr[...] + o_inter
    o = _rms(o) * nw_ref[...]
    o_ref[...] = o * _silu(g_ref[...])
    b_end = b[C - 1:C]
    eye = _iota((HG_HD, HG_HD), 0) == _iota((HG_HD, HG_HD), 1)
    col = jnp.sum(jnp.where(eye, jnp.exp(b_end), 0.0), axis=1, keepdims=True)
    kd = k * jnp.exp(b_end - b)
    s_new = col * s_prev + _dot_tn(kd.astype(BF16), v.astype(BF16))
    s_scr[...] = s_new

    @pl.when(c == pl.num_programs(2) - 1)
    def _():
        sout_ref[0, 0] = s_new


def _hgrn(zsrc, col0, nb, nchunks, lb, nw, s0, t_valid):
    C = HG_C

    def cspec(off):
        return pl.BlockSpec((C, LANE), lambda b, h, c, off=off: (b * nchunks + c, col0 + off + h))

    ins = [zsrc, zsrc, zsrc, zsrc, lb, nw]
    specs = [cspec(0), cspec(4), cspec(8), cspec(12),
             pl.BlockSpec((1, LANE), lambda b, h, c: (0, h)), pl.BlockSpec((1, LANE), lambda b, h, c: (0, 0))]
    if s0 is not None:
        ins.append(s0)
        specs.append(pl.BlockSpec((1, 1, HG_HD, HG_HD), lambda b, h, c: (b, h, 0, 0)))
    return pl.pallas_call(
        functools.partial(_hgrn_kernel, t_valid=t_valid, has_s0=s0 is not None),
        out_shape=(jax.ShapeDtypeStruct((nb * nchunks * C, HG_W), F32),
                   jax.ShapeDtypeStruct((nb, HG_H, HG_HD, HG_HD), F32)),
        grid=(nb, HG_H, nchunks),
        in_specs=specs,
        out_specs=(pl.BlockSpec((C, LANE), lambda b, h, c: (b * nchunks + c, h)),
                   pl.BlockSpec((1, 1, HG_HD, HG_HD), lambda b, h, c: (b, h, 0, 0))),
        scratch_shapes=[pltpu.VMEM((HG_HD, HG_HD), F32), pltpu.VMEM((C, HG_HD), F32)],
        compiler_params=_cp(("parallel", "parallel", "arbitrary")),
        name="hgrn",
    )(*ins)


S5_TILE = 128
S5_ST = 512
S5_SEG = 8


def _cpow(ar, ai, n):
    rr, ri = None, None
    br, bi = ar, ai
    while n:
        if n & 1:
            if rr is None:
                rr, ri = br, bi
            else:
                rr, ri = rr * br - ri * bi, rr * bi + ri * br
        n >>= 1
        if n:
            br, bi = br * br - bi * bi, 2.0 * br * bi
    return rr, ri


def _s5p_kernel(u_ref, bw_ref, cw_ref, ar_ref, ai_ref, d_ref, y_ref, hl_ref, xs, *, T):
    nj = T // S5_SEG
    rc = 512
    for r in range(0, T, rc):
        x = _dot(u_ref[0, r:r + rc, :].astype(BF16), bw_ref[0])
        xs[r // S5_SEG:(r + rc) // S5_SEG] = x.reshape(rc // S5_SEG, S5_SEG, 2 * S5_ST)
    ar = jnp.broadcast_to(ar_ref[...], (S5_SEG, S5_ST))
    ai = jnp.broadcast_to(ai_ref[...], (S5_SEG, S5_ST))

    def step(j, carry):
        hr, hi = carry
        xj = xs[j]
        return ar * hr - ai * hi + xj[:, :S5_ST], ar * hi + ai * hr + xj[:, S5_ST:]

    z = jnp.zeros((S5_SEG, S5_ST), F32)
    er, ei = lax.fori_loop(0, nj, step, (z, z), unroll=8)
    pr, pi = _cpow(ar[0:1], ai[0:1], nj)
    rows = _iota((S5_SEG, S5_ST), 0)
    cr = jnp.zeros((1, S5_ST), F32)
    ci = jnp.zeros((1, S5_ST), F32)
    ir, ii = z, z
    for s in range(S5_SEG):
        ir = jnp.where(rows == s, cr, ir)
        ii = jnp.where(rows == s, ci, ii)
        cr, ci = pr * cr - pi * ci + er[s:s + 1], pr * ci + pi * cr + ei[s:s + 1]

    def step2(j, carry):
        hr, hi = step(j, carry)
        xs[j] = jnp.concatenate([hr, hi], axis=1)
        return hr, hi

    hr, hi = lax.fori_loop(0, nj, step2, (ir, ii), unroll=8)
    hl_ref[0] = jnp.concatenate([hr, hi], axis=1)
    for r in range(0, T, rc):
        h = xs[r // S5_SEG:(r + rc) // S5_SEG].reshape(rc, 2 * S5_ST)
        u = u_ref[0, r:r + rc, :]
        y_ref[0, r:r + rc, :] = _gelu(_dot(h.astype(BF16), cw_ref[0]) + d_ref[...] * u)


def _s5_prompt(u, bw, cw, ar, ai, d):
    nb, T, _ = u.shape
    nt = SSM_W // S5_TILE
    return pl.pallas_call(
        functools.partial(_s5p_kernel, T=T),
        out_shape=(jax.ShapeDtypeStruct((nb, T, SSM_W), F32),
                   jax.ShapeDtypeStruct((nb, S5_SEG, nt * 2 * S5_ST), F32)),
        grid=(nb, nt),
        in_specs=[pl.BlockSpec((1, T, S5_TILE), lambda b, j: (b, 0, j)),
                  pl.BlockSpec((1, S5_TILE, 2 * S5_ST), lambda b, j: (j, 0, 0)),
                  pl.BlockSpec((1, 2 * S5_ST, S5_TILE), lambda b, j: (j, 0, 0)),
                  pl.BlockSpec((1, S5_ST), lambda b, j: (0, j)),
                  pl.BlockSpec((1, S5_ST), lambda b, j: (0, j)),
                  pl.BlockSpec((1, S5_TILE), lambda b, j: (0, j))],
        out_specs=(pl.BlockSpec((1, T, S5_TILE), lambda b, j: (b, 0, j)),
                   pl.BlockSpec((1, S5_SEG, 2 * S5_ST), lambda b, j: (b, 0, j))),
        scratch_shapes=[pltpu.VMEM((T // S5_SEG, S5_SEG, 2 * S5_ST), F32)],
        compiler_params=_cp(("parallel", "parallel")),
        name="s5_prompt",
    )(u, bw, cw, ar, ai, d)


def _s5s_kernel(u_ref, bw_ref, cw_ref, ar_ref, ai_ref, d_ref, h0r_ref, h0i_ref, y_ref, hr_ref, hi_ref, *, T):
    ar, ai = ar_ref[...], ai_ref[...]
    hr, hi = h0r_ref[...], h0i_ref[...]
    for t in range(T):
        u = u_ref[t]
        x = _dot(u.astype(BF16), bw_ref[0])
        hr, hi = ar * hr - ai * hi + x[:, :S5_ST], ar * hi + ai * hr + x[:, S5_ST:]
        h = jnp.concatenate([hr, hi], axis=1)
        y_ref[t] = _gelu(_dot(h.astype(BF16), cw_ref[0]) + d_ref[...] * u)
    hr_ref[...] = hr
    hi_ref[...] = hi


def _s5_sample(u, bw, cw, ar, ai, d, h0r, h0i):
    T, nb, _ = u.shape
    nt = SSM_W // S5_TILE
    st = pl.BlockSpec((nb, S5_ST), lambda j: (0, j))
    return pl.pallas_call(
        functools.partial(_s5s_kernel, T=T),
        out_shape=(jax.ShapeDtypeStruct((T, nb, SSM_W), F32),
                   jax.ShapeDtypeStruct((nb, nt * S5_ST), F32), jax.ShapeDtypeStruct((nb, nt * S5_ST), F32)),
        grid=(nt,),
        in_specs=[pl.BlockSpec((T, nb, S5_TILE), lambda j: (0, 0, j)),
                  pl.BlockSpec((1, S5_TILE, 2 * S5_ST), lambda j: (j, 0, 0)),
                  pl.BlockSpec((1, 2 * S5_ST, S5_TILE), lambda j: (j, 0, 0)),
                  pl.BlockSpec((1, S5_ST), lambda j: (0, j)), pl.BlockSpec((1, S5_ST), lambda j: (0, j)),
                  pl.BlockSpec((1, S5_TILE), lambda j: (0, j)), st, st],
        out_specs=(pl.BlockSpec((T, nb, S5_TILE), lambda j: (0, 0, j)), st, st),
        compiler_params=_cp(("parallel",)),
        name="s5_sample",
    )(u, bw, cw, ar, ai, d, h0r, h0i)


def _s5_params(lam_re, lam_im, log_dt, b_re, b_im, c_re, c_im):
    dt = jnp.exp(log_dt)[:, None]
    mag = jnp.exp(lam_re * dt)
    ar, ai = mag * jnp.cos(lam_im * dt), mag * jnp.sin(lam_im * dt)
    den = lam_re * lam_re + lam_im * lam_im
    fr = ((ar - 1.0) * lam_re + ai * lam_im) / den
    fi = (ai * lam_re - (ar - 1.0) * lam_im) / den
    bbr = fr[..., None] * b_re - fi[..., None] * b_im
    bbi = fr[..., None] * b_im + fi[..., None] * b_re
    nt, gl = SSM_W // S5_TILE, S5_TILE // SSM_CH
    eye = jnp.eye(gl, dtype=F32)

    def bblk(m):
        m = m.reshape(nt, gl, SSM_P, SSM_CH)
        return jnp.einsum("jgpc,gh->jgchp", m, eye).reshape(nt, S5_TILE, S5_ST)

    def cblk(m):
        m = m.reshape(nt, gl, SSM_CH, SSM_P)
        return jnp.einsum("jgcp,gh->jgphc", m, eye).reshape(nt, S5_ST, S5_TILE)

    bw = jnp.concatenate([bblk(bbr), bblk(bbi)], axis=2).astype(BF16)
    cw = jnp.concatenate([cblk(c_re), -cblk(c_im)], axis=1).astype(BF16)
    return bw, cw, ar.reshape(1, -1), ai.reshape(1, -1)


def _compress_kernel(seg_ref, w1_ref, pe_ref, w2_ref, o_ref):
    rb, nseg, _ = seg_ref.shape
    seg = seg_ref[...].reshape(rb * nseg, CMP_STRIDE * HD)
    p = _dot(seg.astype(BF16), w1_ref[0])
    p1 = pltpu.roll(p[:, CMP_HID:], shift=rb * nseg - 1, axis=0)
    hid = pe_ref[0] + p[:, :CMP_HID] + p1
    o_ref[...] = _dot(_gelu(hid).astype(BF16), w2_ref[0]).reshape(rb, nseg, HD)


def _compress(segs, w1cat, pe_term, w2, rows_per_type):
    r, nseg, _ = segs.shape
    rb = 8
    tsel = lambda i: ((i * rb) // rows_per_type, 0, 0)
    return pl.pallas_call(
        _compress_kernel,
        out_shape=jax.ShapeDtypeStruct((r, nseg, HD), F32),
        grid=(r // rb,),
        in_specs=[pl.BlockSpec((rb, nseg, CMP_STRIDE * HD), lambda i: (i, 0, 0)),
                  pl.BlockSpec((1, CMP_STRIDE * HD, 2 * CMP_HID), tsel),
                  pl.BlockSpec((1, 1, CMP_HID), tsel),
                  pl.BlockSpec((1, CMP_HID, HD), tsel)],
        out_specs=pl.BlockSpec((rb, nseg, HD), lambda i: (i, 0, 0)),
        compiler_params=_cp(("parallel",)),
        name="compress",
    )(segs, w1cat, pe_term, w2)


def _cmp_to_sel(nc, nsel, rows, cols):
    cs = np.arange(nc) * CMP_STRIDE
    ss = np.arange(nsel) * SEL_BLOCK
    ov = np.minimum(cs[:, None] + CMP_LEN, ss[None, :] + SEL_BLOCK) - np.maximum(cs[:, None], ss[None, :])
    m = np.zeros((rows, cols), np.float32)
    m[:nc, :nsel] = np.clip(ov, 0, None).astype(np.float32) / CMP_LEN
    return jnp.asarray(m, BF16)


def _topk_mask(score, k):
    lane = _iota(score.shape, 1)
    sel = jnp.zeros(score.shape, F32)
    for _ in range(k):
        m = jnp.max(score, axis=-1, keepdims=True)
        j = jnp.min(jnp.where(score == m, lane, score.shape[1]), axis=-1, keepdims=True)
        hit = lane == j
        sel = jnp.where(hit, 1.0, sel)
        score = jnp.where(hit, -jnp.inf, score)
    return sel


def _sel_score(imp, tpos, nsel):
    blk = _iota(imp.shape, 1)
    cur = jnp.right_shift(tpos, 6)
    forced = (blk == 0) | (blk == cur) | (blk == cur - 1)
    allowed = blk * SEL_BLOCK <= tpos
    score = jnp.where(forced, GRP + 1.0, jnp.where(allowed, imp, -1.0))
    return jnp.where(blk < nsel, score, -jnp.inf)


def _lane_masks(shape):
    lane = _iota(shape, 1)
    return (lane < HD).astype(F32), (lane >= HD).astype(F32)


def _cmpsel_kernel(q_ref, kc_ref, vc_ref, c2s_ref, o_ref, sel_ref, *, tq, nseg, nsel):
    i = pl.program_id(1)
    tpos = i * tq + _iota((tq, 1), 0)
    ok = (_iota((tq, nseg), 1) * CMP_STRIDE + (CMP_LEN - 1)) <= tpos
    mlo, mhi = _lane_masks((tq, LANE))
    scale = HD ** -0.5
    for kh in range(KVH):
        kc = kc_ref[0, kh].astype(BF16)
        vc = vc_ref[0, kh].astype(BF16)
        imp = jnp.zeros((tq, nseg), F32)
        for pp in range(2):
            cb = (2 * kh + pp) * LANE
            q2 = q_ref[:, cb:cb + LANE] * scale
            acc = jnp.zeros((tq, LANE), F32)
            for msk in (mlo, mhi):
                lg = jnp.where(ok, _dot_nt((q2 * msk).astype(BF16), kc), NEG)
                e = jnp.exp(lg - jnp.max(lg, axis=-1, keepdims=True))
                p = jnp.where(ok, e / jnp.sum(e, axis=-1, keepdims=True), 0.0)
                imp = imp + p
                acc = acc + _dot(p.astype(BF16), vc) * msk
            o_ref[:, cb:cb + LANE] = acc
        imps = _dot(imp.astype(BF16), c2s_ref[...])
        sel_ref[0, kh] = _topk_mask(_sel_score(imps, tpos, nsel), SEL_TOPK)


def _cmpsel_prompt(z, kc_sym, vc_sym, c2s):
    tq = 256
    nq = T_P // tq
    nseg = kc_sym.shape[2]
    return pl.pallas_call(
        functools.partial(_cmpsel_kernel, tq=tq, nseg=nseg, nsel=T_P // SEL_BLOCK),
        out_shape=(jax.ShapeDtypeStruct((NP, NSA_W), F32), jax.ShapeDtypeStruct((B_P, KVH, T_P, LANE), F32)),
        grid=(B_P, nq),
        in_specs=[pl.BlockSpec((tq, NSA_W), lambda b, i: (b * nq + i, 0)),
                  pl.BlockSpec((1, KVH, nseg, LANE), lambda b, i: (b, 0, 0, 0)),
                  pl.BlockSpec((1, KVH, nseg, LANE), lambda b, i: (b, 0, 0, 0)),
                  pl.BlockSpec((nseg, LANE), lambda b, i: (0, 0))],
        out_specs=(pl.BlockSpec((tq, NSA_W), lambda b, i: (b * nq + i, 0)),
                   pl.BlockSpec((1, KVH, tq, LANE), lambda b, i: (b, 0, i, 0))),
        compiler_params=_cp(("parallel", "parallel")),
        name="cmpsel_prompt",
    )(z, kc_sym, vc_sym, c2s)


FL_T = 512


def _flash_kernel(*refs, mode):
    if mode == "sel":
        it_ref, jt_ref, first_ref, last_ref, q_ref, k_ref, v_ref, bias_ref, sel_ref, o_ref, m_scr, l_scr, acc_scr = refs
    else:
        it_ref, jt_ref, first_ref, last_ref, q_ref, k_ref, v_ref, bias_ref, o_ref, m_scr, l_scr, acc_scr = refs
    t = FL_T
    s = pl.program_id(2)

    @pl.when(first_ref[s] == 1)
    def _():
        m_scr[...] = jnp.full((2 * t, 1), NEG, F32)
        l_scr[...] = jnp.zeros((2 * t, 1), F32)
        acc_scr[...] = jnp.zeros((2 * t, LANE), F32)

    mlo, mhi = _lane_masks((t, LANE))
    q2 = q_ref[...] * (HD ** -0.5)
    qs = jnp.concatenate([q2 * mlo, q2 * mhi], axis=0).astype(BF16)
    lg = _dot_nt(qs, k_ref[0, 0].astype(BF16)) + bias_ref[0, 0].reshape(2 * t, t)
    if mode == "sel":
        blk = jt_ref[s] * (t // SEL_BLOCK) + jnp.right_shift(_iota((LANE, t), 1), 6)
        expand = (_iota((LANE, t), 0) == blk).astype(BF16)
        drop = ((sel_ref[0, 0] - 1.0) * 1e30).astype(BF16)
        mb = _dot(drop, expand)
        lg = lg + jnp.concatenate([mb, mb], axis=0)
    m_old = m_scr[...]
    m_new = jnp.maximum(m_old, jnp.max(lg, axis=-1, keepdims=True))
    alpha = jnp.exp(m_old - m_new)
    p = jnp.exp(lg - m_new)
    l_scr[...] = alpha * l_scr[...] + jnp.sum(p, axis=-1, keepdims=True)
    acc_scr[...] = alpha * acc_scr[...] + _dot(p.astype(BF16), v_ref[0, 0].astype(BF16))
    m_scr[...] = m_new

    @pl.when(last_ref[s] == 1)
    def _():
        o = acc_scr[...] / l_scr[...]
        o_ref[...] = o[:t] * mlo + o[t:] * mhi


def _flash_prompt(z, ksym, vsym, bias, sel, mode):
    t = FL_T
    nq = T_P // t
    back = nq if mode == "sel" else -(-WINDOW // t)
    pairs = [(i, j) for i in range(nq) for j in range(max(0, i - back), i + 1)]
    it = jnp.asarray([a for a, _ in pairs], jnp.int32)
    jt = jnp.asarray([b for _, b in pairs], jnp.int32)
    first = jnp.asarray([int(b == max(0, a - back)) for a, b in pairs], jnp.int32)
    last = jnp.asarray([int(a == b) for a, b in pairs], jnp.int32)
    noff = bias.shape[1]
    qspec = pl.BlockSpec((t, LANE), lambda b, hp, s, it, jt, fi, la: (b * nq + it[s], hp))
    kvspec = pl.BlockSpec((1, 1, t, LANE), lambda b, hp, s, it, jt, fi, la: (b, hp // 2, jt[s], 0))
    ins = [z, ksym, vsym, bias]
    specs = [qspec, kvspec, kvspec,
             pl.BlockSpec((1, 1, 2, t, t),
                          lambda b, hp, s, it, jt, fi, la: (hp, jnp.minimum(it[s] - jt[s], noff - 1), 0, 0, 0))]
    if mode == "sel":
        ins.append(sel)
        specs.append(pl.BlockSpec((1, 1, t, LANE), lambda b, hp, s, it, jt, fi, la: (b, hp // 2, it[s], 0)))
    gs = pltpu.PrefetchScalarGridSpec(
        num_scalar_prefetch=4,
        grid=(B_P, NH // 2, len(pairs)),
        in_specs=specs,
        out_specs=qspec,
        scratch_shapes=[pltpu.VMEM((2 * t, 1), F32), pltpu.VMEM((2 * t, 1), F32), pltpu.VMEM((2 * t, LANE), F32)],
    )
    return pl.pallas_call(
        functools.partial(_flash_kernel, mode=mode),
        out_shape=jax.ShapeDtypeStruct((NP, NSA_W), F32),
        grid_spec=gs,
        compiler_params=_cp(("parallel", "parallel", "arbitrary")),
        name="flash_" + mode,
    )(it, jt, first, last, *ins)


def _rel_bucket_np(dist):
    exact = REL_BUCKETS // 2
    d = np.maximum(dist, 0)
    logd = np.log(np.maximum(d, 1).astype(np.float32) / exact) / math.log(REL_MAX_DIST / exact)
    far = np.minimum(exact + (logd * (REL_BUCKETS - exact)).astype(np.int32), REL_BUCKETS - 1)
    return np.where(d < exact, d, far)


def _prompt_bias(rel_bias, mode):
    t = FL_T
    noff = 3 if mode == "sel" else -(-WINDOW // t) + 1
    dist = (np.arange(noff)[:, None, None] * t + np.arange(t)[None, :, None]) - np.arange(t)[None, None, :]
    ok = dist >= 0 if mode == "sel" else (dist >= 0) & (dist <= WINDOW)
    tb = rel_bias.astype(F32)[jnp.asarray(_rel_bucket_np(dist))]
    tb = jnp.where(jnp.asarray(ok)[..., None], tb, NEG)
    return tb.transpose(3, 0, 1, 2).reshape(NH // 2, 2, noff, t, t).transpose(0, 2, 1, 3, 4)


S_KEYS = PAST + LANE
W_KEYS = WINDOW + LANE
S_ROWS = KVH * T_S * GRP


def _softmax_rows(lg):
    e = jnp.exp(lg - jnp.max(lg, axis=-1, keepdims=True))
    return e / jnp.sum(e, axis=-1, keepdims=True)


def _sattn_kernel(pt_ref, q_ref, kc_ref, vc_ref, new_ref, wc_ref, bsel_ref, bwin_ref, c2s_ref, gsum_ref, esel_ref,
                  *rest, nseg, nsel, pos0):
    pages = rest[:N_PAGES]
    ocmp_ref, osel_ref, owin_ref = rest[N_PAGES:]
    q = q_ref[0].astype(BF16)
    r = _iota((S_ROWS, 1), 0)
    tpos = pos0 + (jnp.right_shift(r, 2) & (T_S - 1))
    ok = (_iota((S_ROWS, nseg), 1) * CMP_STRIDE + (CMP_LEN - 1)) <= tpos
    lg = jnp.where(ok, _dot_nt(q, kc_ref[0].astype(BF16)), NEG)
    p = jnp.where(ok, _softmax_rows(lg), 0.0)
    ocmp_ref[0] = _dot(p.astype(BF16), vc_ref[0].astype(BF16))
    imp = _dot(gsum_ref[...], p.astype(BF16))
    imps = _dot(imp.astype(BF16), c2s_ref[...])
    sel = _topk_mask(_sel_score(imps, tpos, nsel), SEL_TOPK)
    new = new_ref[0]
    lgs = [_dot_nt(q, pg[0, :, 2 * KV_W:3 * KV_W].astype(BF16)) for pg in pages]
    lgs.append(_dot_nt(q, new[:, 0:KV_W].astype(BF16)))
    lg = jnp.concatenate(lgs, axis=1) + bsel_ref[...]
    valid = _dot(sel.astype(BF16), esel_ref[...]) > 0.5
    p = _softmax_rows(jnp.where(valid, lg, NEG)).astype(BF16)
    o = _dot(p[:, PAST:], new[:, KV_W:2 * KV_W].astype(BF16))
    for n, pg in enumerate(pages):
        o = o + _dot(p[:, n * PAGE:(n + 1) * PAGE], pg[0, :, 3 * KV_W:4 * KV_W].astype(BF16))
    osel_ref[0] = o
    wc = wc_ref[0]
    lg = jnp.concatenate([_dot_nt(q, wc[:, :KV_W].astype(BF16)), _dot_nt(q, new[:, 2 * KV_W:3 * KV_W].astype(BF16))],
                         axis=1) + bwin_ref[...]
    p = _softmax_rows(lg).astype(BF16)
    owin_ref[0] = _dot(p[:, :WINDOW], wc[:, KV_W:].astype(BF16)) + _dot(p[:, WINDOW:], new[:, 3 * KV_W:].astype(BF16))


def _sample_attn(page_table, qs, kc_all, vc_all, newkv, wcache, bsel, bwin, c2s, gsum, esel, cache):
    nseg = kc_all.shape[1]
    nsel = -(-(PAST + T_S) // SEL_BLOCK)
    seq3 = lambda a: pl.BlockSpec((1,) + a.shape[1:], lambda s, pt: (s, 0, 0))
    const2 = lambda a: pl.BlockSpec(a.shape, lambda s, pt: (0, 0))
    page_specs = [pl.BlockSpec((1, PAGE, 4 * KV_W), lambda s, pt, n=n: (pt[s, n], 0, 0)) for n in range(N_PAGES)]
    oshape = jax.ShapeDtypeStruct((B_S, S_ROWS, KV_W), F32)
    ospec = pl.BlockSpec((1, S_ROWS, KV_W), lambda s, pt: (s, 0, 0))
    gs = pltpu.PrefetchScalarGridSpec(
        num_scalar_prefetch=1,
        grid=(B_S,),
        in_specs=[seq3(qs), seq3(kc_all), seq3(vc_all), seq3(newkv), seq3(wcache), const2(bsel), const2(bwin),
                  const2(c2s), const2(gsum), const2(esel)] + page_specs,
        out_specs=(ospec, ospec, ospec),
    )
    return pl.pallas_call(
        functools.partial(_sattn_kernel, nseg=nseg, nsel=nsel, pos0=PAST),
        out_shape=(oshape, oshape, oshape),
        grid_spec=gs,
        compiler_params=_cp(("parallel",)),
        name="sample_attn",
    )(page_table, qs, kc_all, vc_all, newkv, wcache, bsel, bwin, c2s, gsum, esel, *([cache] * N_PAGES))


def _sample_bias(rel_bias):
    r = np.arange(S_ROWS)
    head = (r // (T_S * GRP)) * GRP + (r % GRP)
    t = (r // GRP) % T_S
    tb = rel_bias.astype(F32)
    s = np.arange(S_KEYS)
    dist = (PAST + t)[:, None] - s[None, :]
    ok = (dist >= 0) & (s[None, :] < PAST + T_S)
    bsel = jnp.where(jnp.asarray(ok), tb[jnp.asarray(_rel_bucket_np(dist)), jnp.asarray(head)[:, None]], NEG)
    w = np.arange(W_KEYS)
    dist = (t + WINDOW)[:, None] - w[None, :]
    ok = (dist >= 0) & (dist <= WINDOW) & (w[None, :] < WINDOW + T_S)
    bwin = jnp.where(jnp.asarray(ok), tb[jnp.asarray(_rel_bucket_np(dist)), jnp.asarray(head)[:, None]], NEG)
    return bsel, bwin


def _mixout_kernel(oh_ref, os_ref, oc_ref, ol_ref, ow_ref, z_ref, gexp_ref, x_ref, w_ref, gain_ref,
                   g1p_ref, g1s_ref, scp_ref, scs_ref, shp_ref, shs_ref, xo_ref, h_ref, *, npt):
    i = pl.program_id(0)
    g1, g2, g3 = _split3(z_ref[...])
    ge = gexp_ref[...]
    gate = _sigmoid(_dot(g1, ge) + _dot(g2, ge) + _dot(g3, ge))
    on = gate[:, :NSA_W] * oc_ref[...] + gate[:, NSA_W:2 * NSA_W] * ol_ref[...] + gate[:, 2 * NSA_W:] * ow_ref[...]
    lhs = jnp.concatenate([oh_ref[...], os_ref[...], on], axis=1).astype(BF16)
    mix = _dot(lhs, w_ref[...])
    xn = x_ref[...] + _pick_mod(i, npt, g1p_ref, g1s_ref) * mix
    xo_ref[...] = xn
    y = _rms(xn) * gain_ref[...]
    h_ref[...] = y * (1.0 + _pick_mod(i, npt, scp_ref, scs_ref)) + _pick_mod(i, npt, shp_ref, shs_ref)


def _mixout(oh, os_, oc, ol, ow, z, gexp, x, w, gain, g1p, g1s, scp, scs, shp, shs):
    tm = 256
    pspec, sspec, npt = _mod_specs(tm)
    row = lambda n: pl.BlockSpec((tm, n), lambda i: (i, 0))
    full = lambda a: pl.BlockSpec(a.shape, lambda i: (0, 0))
    return pl.pallas_call(
        functools.partial(_mixout_kernel, npt=npt),
        out_shape=(jax.ShapeDtypeStruct((NTOK, D), F32), jax.ShapeDtypeStruct((NTOK, D), F32)),
        grid=(NTOK // tm,),
        in_specs=[row(HG_W), row(SSM_W), row(NSA_W), row(NSA_W), row(NSA_W),
                  pl.BlockSpec((tm, LANE), lambda i: (i, C_NG // LANE)), full(gexp), row(D), full(w), full(gain),
                  pspec, sspec, pspec, sspec, pspec, sspec],
        out_specs=(row(D), row(D)),
        compiler_params=_cp(("parallel",)),
        name="mixout",
    )(oh, os_, oc, ol, ow, z, gexp, x, w, gain, g1p, g1s, scp, scs, shp, shs)


MOE_TM = 256
N_TILES = NTOK * TOP_K // MOE_TM + N_EXP
N_SLOTS = N_TILES * MOE_TM


def _router_kernel(h_ref, w_ref, b_ref, idx_ref, wt_ref):
    tm = h_ref.shape[0]
    h1, h2, h3 = _split3(h_ref[...])
    w1, w2, w3 = w_ref[0], w_ref[1], w_ref[2]
    logits = _dot(h1, w1) + (_dot(h1, w2) + _dot(h2, w1)) + (_dot(h1, w3) + _dot(h2, w2) + _dot(h3, w1))
    scores = _sigmoid(logits)
    lane = _iota((tm, LANE), 1)
    s = jnp.where(lane < N_EXP, scores + b_ref[...], -jnp.inf)
    idx = jnp.zeros((tm, LANE), jnp.int32)
    wt = jnp.zeros((tm, LANE), F32)
    for r in range(TOP_K):
        m = jnp.max(s, axis=-1, keepdims=True)
        j = jnp.min(jnp.where(s == m, lane, LANE), axis=-1, keepdims=True)
        hit = lane == j
        wsel = jnp.sum(jnp.where(hit, scores, 0.0), axis=-1, keepdims=True)
        idx = jnp.where(lane == r, j, idx)
        wt = jnp.where(lane == r, wsel, wt)
        s = jnp.where(hit, -jnp.inf, s)
    wt = wt / jnp.sum(wt, axis=-1, keepdims=True) * ROUTED_SCALE
    idx_ref[...] = idx
    wt_ref[...] = wt


def _router(h, w, b):
    tm = 512
    return pl.pallas_call(
        _router_kernel,
        out_shape=(jax.ShapeDtypeStruct((NTOK, LANE), jnp.int32), jax.ShapeDtypeStruct((NTOK, LANE), F32)),
        grid=(NTOK // tm,),
        in_specs=[pl.BlockSpec((tm, D), lambda i: (i, 0)), pl.BlockSpec((3, D, LANE), lambda i: (0, 0, 0)),
                  pl.BlockSpec((1, LANE), lambda i: (0, 0))],
        out_specs=(pl.BlockSpec((tm, LANE), lambda i: (i, 0)), pl.BlockSpec((tm, LANE), lambda i: (i, 0))),
        compiler_params=_cp(("parallel",)),
        name="router",
    )(h, w, b)


def _row_copy(src_hbm, row, dst, slot, sem):
    return pltpu.make_async_copy(src_hbm.at[pl.ds(row, 1), :], dst.at[pl.ds(slot, 1), :], sem)


def _experts_kernel(texp_ref, nvalid_ref, tok_ref, x_hbm, wg_ref, wu_ref, wd_ref, y_ref, xbuf, sem):
    i = pl.program_id(0)
    tm = MOE_TM

    @pl.when(i < nvalid_ref[0])
    def _():
        def issue(r, c):
            _row_copy(x_hbm, tok_ref[0, 0, r], xbuf, r, sem).start()
            return c

        lax.fori_loop(0, tm, issue, 0)

        def drain(r, c):
            _row_copy(x_hbm, 0, xbuf, r, sem).wait()
            return c

        lax.fori_loop(0, tm, drain, 0)
        x = xbuf[...].astype(BF16)
        hmid = _silu(_dot(x, wg_ref[0].astype(BF16))) * _dot(x, wu_ref[0].astype(BF16))
        y_ref[...] = _dot(hmid.astype(BF16), wd_ref[0].astype(BF16))

    @pl.when(i >= nvalid_ref[0])
    def _():
        y_ref[...] = jnp.zeros((tm, D), F32)


def _experts(tile_exp, nvalid, slot_tok, h, wg, wu, wd):
    tm = MOE_TM
    gs = pltpu.PrefetchScalarGridSpec(
        num_scalar_prefetch=2,
        grid=(N_TILES,),
        in_specs=[pl.BlockSpec((1, 1, tm), lambda i, te, nv: (i, 0, 0), memory_space=pltpu.SMEM),
                  pl.BlockSpec(memory_space=pl.ANY),
                  pl.BlockSpec((1, D, D_EXP), lambda i, te, nv: (te[i], 0, 0)),
                  pl.BlockSpec((1, D, D_EXP), lambda i, te, nv: (te[i], 0, 0)),
                  pl.BlockSpec((1, D_EXP, D), lambda i, te, nv: (te[i], 0, 0))],
        out_specs=pl.BlockSpec((tm, D), lambda i, te, nv: (i, 0)),
        scratch_shapes=[pltpu.VMEM((tm, D), F32), pltpu.SemaphoreType.DMA(())],
    )
    return pl.pallas_call(
        _experts_kernel,
        out_shape=jax.ShapeDtypeStruct((N_SLOTS, D), F32),
        grid_spec=gs,
        compiler_params=_cp(("arbitrary",)),
        name="experts",
    )(tile_exp, nvalid, slot_tok, h, wg, wu, wd)


CMB_TM = 128


def _combine_kernel(pos_ref, y_hbm, wt_ref, h_ref, x_ref, wg_ref, wu_ref, wd_ref, g2p_ref, g2s_ref, fin_ref,
                    xo_ref, yo_ref, ybuf, sem, *, npt):
    i = pl.program_id(0)
    tm = CMB_TM

    def issue(r, c):
        for kk in range(TOP_K):
            _row_copy(y_hbm, pos_ref[0, 0, r * TOP_K + kk], ybuf.at[kk], r, sem).start()
        return c

    lax.fori_loop(0, tm, issue, 0)
    hb = h_ref[...].astype(BF16)
    shared = _dot((_silu(_dot(hb, wg_ref[...])) * _dot(hb, wu_ref[...])).astype(BF16), wd_ref[...])

    def drain(r, c):
        for kk in range(TOP_K):
            _row_copy(y_hbm, 0, ybuf.at[kk], r, sem).wait()
        return c

    lax.fori_loop(0, tm, drain, 0)
    wt = wt_ref[...]
    routed = ybuf[0] * wt[:, 0:1]
    for kk in range(1, TOP_K):
        routed = routed + ybuf[kk] * wt[:, kk:kk + 1]
    xn = x_ref[...] + _pick_mod(i, npt, g2p_ref, g2s_ref) * (routed + shared)
    xo_ref[...] = xn
    yo_ref[...] = _rms(xn) * fin_ref[...]


def _combine(pos, y_slots, wts, h, x, wg, wu, wd, g2p, g2s, fin):
    tm = CMB_TM
    pspec, sspec, npt = _mod_specs(tm)
    row = pl.BlockSpec((tm, D), lambda i: (i, 0))
    full = lambda a: pl.BlockSpec(a.shape, lambda i: (0, 0))
    return pl.pallas_call(
        functools.partial(_combine_kernel, npt=npt),
        out_shape=(jax.ShapeDtypeStruct((NTOK, D), F32), jax.ShapeDtypeStruct((NTOK, D), F32)),
        grid=(NTOK // tm,),
        in_specs=[pl.BlockSpec((1, 1, tm * TOP_K), lambda i: (i, 0, 0), memory_space=pltpu.SMEM),
                  pl.BlockSpec(memory_space=pl.ANY), pl.BlockSpec((tm, LANE), lambda i: (i, 0)), row, row, full(wg), full(wu), full(wd), pspec, sspec, full(fin)],
        out_specs=(row, row),
        scratch_shapes=[pltpu.VMEM((TOP_K, tm, D), F32), pltpu.SemaphoreType.DMA(())],
        compiler_params=_cp(("arbitrary",)),
        name="combine",
    )(pos, y_slots, wts, h, x, wg, wu, wd, g2p, g2s, fin)


def _dispatch(idx):
    n = idx.shape[0]
    onehot = (idx[:, :, None] == jnp.arange(N_EXP, dtype=jnp.int32)[None, None, :]).astype(jnp.int32).sum(axis=1)
    cum = jnp.cumsum(onehot, axis=0) - onehot
    counts = onehot.sum(axis=0)
    padded = (counts + MOE_TM - 1) // MOE_TM * MOE_TM
    pad_end = jnp.cumsum(padded)
    pad_start = pad_end - padded
    dest = (pad_start[idx] + jnp.take_along_axis(cum, idx, axis=1)).astype(jnp.int32)
    flat = dest.reshape(-1)
    tok = jnp.repeat(jnp.arange(n, dtype=jnp.int32), TOP_K)
    slot_tok = jnp.zeros((N_SLOTS,), jnp.int32).at[flat].set(tok)
    nvalid = (pad_end[-1] // MOE_TM).astype(jnp.int32)
    starts = jnp.arange(N_TILES, dtype=jnp.int32) * MOE_TM
    texp = jnp.minimum(jnp.searchsorted(pad_end, starts, side="right"), N_EXP - 1).astype(jnp.int32)
    texp = jnp.where(jnp.arange(N_TILES) < nvalid, texp, texp[jnp.maximum(nvalid - 1, 0)])
    return texp, nvalid.reshape(1), slot_tok.reshape(N_TILES, 1, MOE_TM), dest.reshape(n // CMB_TM, 1, CMB_TM * TOP_K)


def _dup(a):
    return jnp.concatenate([a, a], axis=-1)


def _heads_first(a, nb, t):
    return a.reshape(nb, t, KVH, HD).transpose(0, 2, 1, 3)


def _segments(a, nb, nseg):
    a = a[:, :nseg * CMP_STRIDE].reshape(nb, nseg, CMP_STRIDE, KVH, HD)
    return a.transpose(0, 3, 1, 2, 4).reshape(nb, KVH, nseg, CMP_STRIDE * HD)


def _layer(l, x, c_all, p, consts):
    (cache_kv, page_table, cache_win, state_hgrn, state_ssm) = p["state"]
    mod = _mm(c_all, p["w_ada"][l], p["b_ada"][l][None], pre="silu", tm=B_P + B_S, tn=512, name="adaln")
    chunks = [mod[:, i * D:(i + 1) * D] for i in range(6)]
    mp = [ch[:B_P][:, None, :] for ch in chunks]
    ms = [jnp.tile(ch[B_P:], (T_S, 1)) for ch in chunks]
    sh1, sc1, g1, sh2, sc2, g2 = range(6)

    w = p["w_in"][l]
    w_in = jnp.concatenate([w[:, 2560:3584], w[:, :2560], w[:, 3584:], jnp.zeros((D, Z_COLS - w.shape[1]), F32)],
                           axis=1).astype(BF16)
    z = _inproj(x, p["norm_mix"][l][None], mp[sc1], ms[sc1], mp[sh1], ms[sh1], w_in)
    zs = z[NP:]

    lb = consts["lower"][l][None]
    nw = p["hgrn_norm"][l][None]
    oh_p, hg_p = _hgrn(z, C_HG // LANE, B_P, T_P // HG_C, lb, nw, None, HG_C)
    zh = zs[:, C_HG:C_SU].reshape(T_S, B_S, 4 * HG_W).transpose(1, 0, 2)
    zh = jnp.pad(zh, ((0, 0), (0, HG_C - T_S), (0, 0))).reshape(B_S * HG_C, 4 * HG_W)
    oh_s, hg_s = _hgrn(zh, 0, B_S, 1, lb, nw, state_hgrn[l], T_S)
    oh_s = oh_s.reshape(B_S, HG_C, HG_W)[:, :T_S].transpose(1, 0, 2).reshape(NS, HG_W)

    bw, cw, ar, ai = _s5_params(p["ssm_lambda_re"][l], p["ssm_lambda_im"][l], p["ssm_log_dt"][l],
                                p["ssm_b_re"][l], p["ssm_b_im"][l], p["ssm_c_re"][l], p["ssm_c_im"][l])
    dsk = p["ssm_d"][l][None]
    nj = T_P // S5_SEG
    u_p = z[:NP, C_SU:C_SU + SSM_W].reshape(B_P, S5_SEG, nj, SSM_W).transpose(0, 2, 1, 3).reshape(B_P, T_P, SSM_W)
    y_p, hl = _s5_prompt(u_p, bw, cw, ar, ai, dsk)
    y_p = y_p.reshape(B_P, nj, S5_SEG, SSM_W).transpose(0, 2, 1, 3).reshape(NP, SSM_W)
    hl = hl[:, S5_SEG - 1].reshape(B_P, SSM_W // S5_TILE, 2, S5_ST)
    ss_p = jnp.stack([hl[:, :, 0].reshape(B_P, SSM_G, SSM_P), hl[:, :, 1].reshape(B_P, SSM_G, SSM_P)], axis=-1)
    st = state_ssm[l]
    y_s, hr, hi = _s5_sample(zs[:, C_SU:C_SU + SSM_W].reshape(T_S, B_S, SSM_W), bw, cw, ar, ai, dsk,
                             st[..., 0].reshape(B_S, -1), st[..., 1].reshape(B_S, -1))
    ss_s = jnp.stack([hr.reshape(B_S, SSM_G, SSM_P), hi.reshape(B_S, SSM_G, SSM_P)], axis=-1)
    y_all = jnp.concatenate([y_p, y_s.reshape(NS, SSM_W)], axis=0)
    os_all = _mm(y_all, p["ssm_w_glu"][l].astype(BF16), p["ssm_b_glu"][l][None], epi="glu", tm=512, tn=SSM_W,
                 name="s5_glu")

    w1 = p["cmp_w1"][l].reshape(2, 2, CMP_STRIDE * HD, CMP_HID)
    w1cat = jnp.concatenate([w1[:, 0], w1[:, 1]], axis=2).astype(BF16)
    pe_term = jnp.einsum("tx,txf->tf", p["cmp_pe"][l].reshape(2, -1), p["cmp_w1"][l],
                         precision=lax.Precision.HIGHEST)[:, None, :]
    w2 = p["cmp_w2"][l].astype(BF16)
    nseg_p = T_P // CMP_STRIDE
    kv_p = z[:NP, C_KV:C_KV + 2 * KV_W].reshape(B_P, T_P, 2, KV_W)
    segs_p = jnp.stack([_segments(kv_p[:, :, 0], B_P, nseg_p), _segments(kv_p[:, :, 1], B_P, nseg_p)])
    kvc_p = _compress(segs_p.reshape(-1, nseg_p, CMP_STRIDE * HD), w1cat, pe_term, w2, B_P * KVH)
    kvc_p = kvc_p.reshape(2, B_P, KVH, nseg_p, HD)
    nseg_s = PAST // CMP_STRIDE
    past_c = cache_kv[l][:, :, 0:2][page_table].reshape(B_S, PAST, 2, KV_W)
    segs_s = jnp.stack([_segments(past_c[:, :, 0], B_S, nseg_s), _segments(past_c[:, :, 1], B_S, nseg_s)])
    kvc_s = _compress(segs_s.reshape(-1, nseg_s, CMP_STRIDE * HD), w1cat, pe_term, w2, B_S * KVH)
    kvc_s = kvc_s.reshape(2, B_S, KVH, nseg_s, HD).transpose(0, 1, 3, 2, 4).reshape(2, B_S, nseg_s, KV_W)

    oc_p, selm = _cmpsel_prompt(z, _dup(kvc_p[0]), _dup(kvc_p[1]), consts["c2s_p"])
    kvr = z[:NP, C_KV + 2 * KV_W:C_KV + 6 * KV_W]
    ksel, vsel, kwin, vwin = [_dup(_heads_first(kvr[:, i * KV_W:(i + 1) * KV_W], B_P, T_P)) for i in range(4)]
    ol_p = _flash_prompt(z, ksel, vsel, consts["bias_sel_p"], selm, "sel")
    ow_p = _flash_prompt(z, kwin, vwin, consts["bias_win_p"], None, "win")

    qs = zs[:, C_Q:C_Q + NSA_W].reshape(T_S, B_S, KVH, GRP, HD).transpose(1, 2, 0, 3, 4) * (HD ** -0.5)
    qs = jnp.einsum("skrd,kh->skrhd", qs.reshape(B_S, KVH, T_S * GRP, HD), jnp.eye(KVH, dtype=F32))
    qs = qs.reshape(B_S, S_ROWS, KV_W)
    newkv = zs[:, C_KV + 2 * KV_W:C_KV + 6 * KV_W].reshape(T_S, B_S, 4 * KV_W).transpose(1, 0, 2)
    newkv = jnp.pad(newkv, ((0, 0), (0, LANE - T_S), (0, 0)))
    wcache = cache_win[l].reshape(B_S, WINDOW, 2 * KV_W)
    oc_s, ol_s, ow_s = _sample_attn(page_table, qs, kvc_s[0], kvc_s[1], newkv, wcache, consts["bsel"], consts["bwin"],
                                    consts["c2s_s"], consts["gsum"], consts["esel"],
                                    cache_kv[l].reshape(-1, PAGE, 4 * KV_W))

    def unblock(o):
        o = o.reshape(B_S, KVH, T_S, GRP, KVH, HD)
        o = jnp.stack([o[:, kh, :, :, kh] for kh in range(KVH)], axis=1)
        return o.transpose(2, 0, 1, 3, 4).reshape(NS, NSA_W)

    oc = jnp.concatenate([oc_p, unblock(oc_s)], axis=0)
    ol = jnp.concatenate([ol_p, unblock(ol_s)], axis=0)
    ow = jnp.concatenate([ow_p, unblock(ow_s)], axis=0)
    oh = jnp.concatenate([oh_p, oh_s], axis=0)

    x, h2 = _mixout(oh, os_all, oc, ol, ow, z, consts["gexp"], x, p["w_out"][l].astype(BF16), p["norm_ffn"][l][None],
                    mp[g1], ms[g1], mp[sc2], ms[sc2], mp[sh2], ms[sh2])

    rw = jnp.stack(_split3(jnp.pad(p["router_w"][l], ((0, 0), (0, LANE - N_EXP)))))
    rb = jnp.pad(p["router_bias"][l], (0, LANE - N_EXP))[None]
    idx, wts = _router(h2, rw, rb)
    texp, nvalid, slot_tok, pos = _dispatch(idx[:, :TOP_K])
    y_slots = _experts(texp, nvalid, slot_tok, h2, p["exp_w_gate"][l], p["exp_w_up"][l], p["exp_w_down"][l])
    x, y_fin = _combine(pos, y_slots, wts, h2, x, p["sh_w_gate"][l].astype(BF16), p["sh_w_up"][l].astype(BF16),
                        p["sh_w_down"][l].astype(BF16), mp[g2], ms[g2], p["final_norm"][None])

    rows_p = z[:NP, C_KV:C_KV + 4 * KV_W].reshape(B_P, T_P, 4, KVH, HD)
    win_p = z[:NP, C_KV + 4 * KV_W:C_KV + 6 * KV_W].reshape(B_P, T_P, 2, KVH, HD)[:, T_P - WINDOW:]
    rows_s = zs[:, C_KV:C_KV + 4 * KV_W].reshape(T_S, B_S, 4, KVH, HD).transpose(1, 0, 2, 3, 4)
    new_win = zs[:, C_KV + 4 * KV_W:C_KV + 6 * KV_W].reshape(T_S, B_S, 2, KVH, HD).transpose(1, 0, 2, 3, 4)
    win_s = jnp.concatenate([cache_win[l][:, T_S:], new_win], axis=1)
    return x, y_fin, (rows_p, win_p, hg_p, ss_p, rows_s, win_s, hg_s, ss_s)


def kernel(x_prompt, x_sample, c_prompt, c_sample, cache_kv, page_table, cache_win, state_hgrn, state_ssm, w_ada, b_ada, norm_mix, norm_ffn, w_in, w_out, hgrn_lb_logits, hgrn_norm, ssm_lambda_re, ssm_lambda_im, ssm_log_dt, ssm_b_re, ssm_b_im, ssm_c_re, ssm_c_im, ssm_d, ssm_w_glu, ssm_b_glu, cmp_pe, cmp_w1, cmp_w2, rel_bias, router_w, router_bias, exp_w_gate, exp_w_up, exp_w_down, sh_w_gate, sh_w_up, sh_w_down, final_norm):
    p = dict(w_ada=w_ada, b_ada=b_ada, norm_mix=norm_mix, norm_ffn=norm_ffn, w_in=w_in, w_out=w_out,
             hgrn_norm=hgrn_norm, ssm_lambda_re=ssm_lambda_re, ssm_lambda_im=ssm_lambda_im, ssm_log_dt=ssm_log_dt,
             ssm_b_re=ssm_b_re, ssm_b_im=ssm_b_im, ssm_c_re=ssm_c_re, ssm_c_im=ssm_c_im, ssm_d=ssm_d,
             ssm_w_glu=ssm_w_glu, ssm_b_glu=ssm_b_glu, cmp_pe=cmp_pe, cmp_w1=cmp_w1, cmp_w2=cmp_w2,
             router_w=router_w, router_bias=router_bias, exp_w_gate=exp_w_gate, exp_w_up=exp_w_up,
             exp_w_down=exp_w_down, sh_w_gate=sh_w_gate, sh_w_up=sh_w_up, sh_w_down=sh_w_down,
             final_norm=final_norm, state=(cache_kv, page_table, cache_win, state_hgrn, state_ssm))
    p_lb = jax.nn.softmax(hgrn_lb_logits.astype(F32), axis=0)
    bsel, bwin = _sample_bias(rel_bias)
    nsel_s = -(-(PAST + T_S) // SEL_BLOCK)
    keys = np.arange(S_KEYS)
    esel = (np.arange(LANE)[:, None] == (keys // SEL_BLOCK)[None, :]) & (keys[None, :] < nsel_s * SEL_BLOCK)
    rr = np.arange(S_ROWS)
    gexp = (np.arange(LANE)[:, None] == (np.arange(3 * NSA_W) // HD)[None, :])
    consts = dict(
        lower=jnp.cumsum(p_lb, axis=0) - p_lb[0],
        c2s_p=_cmp_to_sel(T_P // CMP_STRIDE - 1, T_P // SEL_BLOCK, T_P // CMP_STRIDE, LANE),
        c2s_s=_cmp_to_sel(PAST // CMP_STRIDE - 1, nsel_s, PAST // CMP_STRIDE, LANE),
        bias_sel_p=_prompt_bias(rel_bias, "sel"), bias_win_p=_prompt_bias(rel_bias, "win"), bsel=bsel, bwin=bwin,
        esel=jnp.asarray(esel, BF16),
        gsum=jnp.asarray(rr[:, None] // GRP == rr[None, :] // GRP, BF16),
        gexp=jnp.asarray(gexp, BF16),
    )
    x = jnp.concatenate([x_prompt.reshape(NP, D), x_sample.transpose(1, 0, 2).reshape(NS, D)], axis=0)
    c_all = jnp.concatenate([c_prompt, c_sample], axis=0)
    outs = []
    y_fin = None
    for l in range(DEPTH):
        x, y_fin, o = _layer(l, x, c_all, p, consts)
        outs.append(o)
    y_prompt = y_fin[:NP].reshape(B_P, T_P, D)
    y_sample = y_fin[NP:].reshape(T_S, B_S, D).transpose(1, 0, 2)
    st = [jnp.stack([o[i] for o in outs]) for i in range(8)]
    return (y_prompt, y_sample) + tuple(st)
```
